```python
import jax, jax.numpy as jnp
from jax import lax
import numpy as np

D_MODEL = 2048
BATCH = 2
SEQ = 4096
DEPTH = 2
DEC_BATCH = 128
DEC_SEQ = 1
PAST_LEN = 2048
PAGE_SIZE = 128

HEAD_DIM = 128
LRU_WIDTH = D_MODEL // 2
LRU_BLOCKS = 8
LRU_BLOCK = LRU_WIDTH // LRU_BLOCKS
CONV_W = 4
LRU_C = 8.0
NSA_HEADS = D_MODEL // 4 // HEAD_DIM
FOX_HEADS = D_MODEL // 4 // HEAD_DIM
D_NSA = NSA_HEADS * HEAD_DIM
D_FOX = FOX_HEADS * HEAD_DIM
D_MIX = LRU_WIDTH + D_NSA + D_FOX
CMP_LEN = 32
CMP_STRIDE = 16
SEL_BLOCK = 64
N_SELECT = 16
WINDOW = 512
FORCE_BONUS = 1e4
ROPE_DIM = HEAD_DIM // 4
ROPE_THETA = 500000.0
D_FF = 4 * D_MODEL
EPS = 1e-6
NEG = -1e30
Q_BLOCK = 128
IN_SIZES = (LRU_WIDTH, LRU_WIDTH, D_NSA, 6 * HEAD_DIM, 3 * NSA_HEADS, D_FOX, D_FOX, D_FOX, FOX_HEADS)
D_IN = sum(IN_SIZES)
SPLIT_POINTS = tuple(int(v) for v in np.cumsum(IN_SIZES)[:-1])

kernel_name = "hymba_lru_nsa_fox_decoder_step"


def rmsnorm(x, g):
    xf = x.astype(jnp.float32)
    y = xf * lax.rsqrt(jnp.mean(xf * xf, axis=-1, keepdims=True) + EPS)
    return (y * g.astype(jnp.float32)).astype(x.dtype)


def rope(x, pos):
    half = ROPE_DIM // 2
    inv = ROPE_THETA ** (-jnp.arange(half, dtype=jnp.float32) / half)
    ang = pos.astype(jnp.float32)[:, None] * inv[None, :]
    cos = jnp.cos(ang)[None, :, None, :].astype(x.dtype)
    sin = jnp.sin(ang)[None, :, None, :].astype(x.dtype)
    x1, x2, rest = x[..., :half], x[..., half:ROPE_DIM], x[..., ROPE_DIM:]
    return jnp.concatenate([x1 * cos - x2 * sin, x2 * cos + x1 * sin, rest], axis=-1)


def masked_softmax(s, mask):
    s = jnp.where(mask, s.astype(jnp.float32), NEG)
    m = jnp.max(s, axis=-1, keepdims=True)
    e = jnp.where(mask, jnp.exp(s - m), 0.0)
    return e / jnp.maximum(jnp.sum(e, axis=-1, keepdims=True), 1e-30)


def gather_pages(pool, page_table):
    g = pool[page_table]
    return g.reshape(g.shape[0], g.shape[1] * g.shape[2], *g.shape[3:])


def causal_conv(x, buf, w, b):
    T = x.shape[1]
    xx = jnp.concatenate([buf.astype(x.dtype), x], axis=1)
    y = b
    for k in range(CONV_W):
        y = y + w[k] * xx[:, k:k + T]
    return y, xx[:, T:]


def blockdiag(x, w, b):
    B_, T = x.shape[:2]
    xb = x.reshape(B_, T, LRU_BLOCKS, LRU_BLOCK)
    return jnp.einsum('btgi,gij->btgj', xb, w).reshape(B_, T, LRU_WIDTH) + b


def rglru(xc, h0, w_rg, b_rg, w_ig, b_ig, lam):
    f32 = jnp.float32
    r = jax.nn.sigmoid(blockdiag(xc, w_rg, b_rg).astype(f32))
    i = jax.nn.sigmoid(blockdiag(xc, w_ig, b_ig).astype(f32))
    log_a = -LRU_C * r * jax.nn.softplus(-lam.astype(f32))
    a = jnp.exp(log_a)
    u = jnp.sqrt(-jnp.expm1(2.0 * log_a)) * (i * xc.astype(f32))

    def step(h, au):
        a_t, u_t = au
        h = a_t * h + u_t
        return h, h

    h_last, hs = lax.scan(step, h0.astype(f32), (jnp.swapaxes(a, 0, 1), jnp.swapaxes(u, 0, 1)))
    return jnp.swapaxes(hs, 0, 1).astype(xc.dtype), h_last.astype(xc.dtype)


def lru_mixer(xb, yb, conv_buf, h0, conv_w, conv_b, w_rg, b_rg, w_ig, b_ig, lam):
    xc, new_buf = causal_conv(xb, conv_buf, conv_w, conv_b)
    hs, h_last = rglru(xc, h0, w_rg, b_rg, w_ig, b_ig, lam)
    return hs * jax.nn.gelu(yb), new_buf, h_last


def nsa_compress(k, v, pe, w_ck, w_cv):
    Tk = k.shape[1]
    n_cmp = (Tk - CMP_LEN) // CMP_STRIDE + 1
    n_slc = -(-Tk // SEL_BLOCK)
    starts = np.arange(n_cmp) * CMP_STRIDE
    idx = starts[:, None] + np.arange(CMP_LEN)[None, :]
    kc = jnp.einsum('bnld,lde->bne', k[:, idx] + pe, w_ck)
    vc = jnp.einsum('bnld,lde->bne', v[:, idx] + pe, w_cv)
    cmp_end = jnp.asarray(starts + CMP_LEN - 1, dtype=jnp.int32)
    j_start = np.arange(n_slc) * SEL_BLOCK
    overlap = (starts[:, None] < j_start[None, :] + SEL_BLOCK) & (starts[:, None] + CMP_LEN > j_start[None, :])
    return kc, vc, cmp_end, jnp.asarray(overlap.astype(np.float32))


def to_blocks(k):
    B_, Tk, d = k.shape
    n_slc = -(-Tk // SEL_BLOCK)
    k = jnp.pad(k, ((0, 0), (0, n_slc * SEL_BLOCK - Tk), (0, 0)))
    return k.reshape(B_, n_slc, SEL_BLOCK, d)


def nsa_attend(q, q_pos, g, kc, vc, cmp_end, overlap, kbs, vbs, kw, vw, w_pos):
    B_, Tq = q.shape[:2]
    scale = HEAD_DIM ** -0.5
    s = jnp.einsum('bqhd,bnd->bqhn', q, kc) * scale
    p_c = masked_softmax(s, (cmp_end[None, :] <= q_pos[:, None])[None, :, None, :])
    o_c = jnp.einsum('bqhn,bnd->bqhd', p_c.astype(vc.dtype), vc)
    imp = jnp.einsum('bqhn,ns->bqs', p_c, overlap)
    n_slc = kbs.shape[1]
    blk = jnp.arange(n_slc, dtype=jnp.int32)[None, :]
    cur = (q_pos // SEL_BLOCK)[:, None]
    forced = (blk == 0) | (blk == cur) | (blk == cur - 1)
    score = jnp.where(blk <= cur, imp + jnp.where(forced, FORCE_BONUS, 0.0), NEG)
    top_s, top_i = lax.top_k(score, min(N_SELECT, n_slc))
    b_idx = jnp.arange(B_)[:, None, None]
    ks = kbs[b_idx, top_i]
    vs = vbs[b_idx, top_i]
    tok_pos = top_i[..., None] * SEL_BLOCK + jnp.arange(SEL_BLOCK, dtype=jnp.int32)
    mask_s = (top_s[..., None] > NEG / 2) & (tok_pos <= q_pos[None, :, None, None])
    n_tok = ks.shape[2] * SEL_BLOCK
    s = jnp.einsum('bqhd,bqkld->bqhkl', q, ks).reshape(B_, Tq, NSA_HEADS, n_tok) * scale
    p_s = masked_softmax(s, mask_s.reshape(B_, Tq, 1, n_tok))
    o_s = jnp.einsum('bqhm,bqmd->bqhd', p_s.astype(vs.dtype), vs.reshape(B_, Tq, n_tok, HEAD_DIM))
    s = jnp.einsum('bqhd,bwd->bqhw', q, kw) * scale
    dist = q_pos[:, None] - w_pos[None, :]
    mask_w = (dist >= 0) & (dist < WINDOW) & (w_pos[None, :] >= 0)
    p_w = masked_softmax(s, mask_w[None, :, None, :])
    o_w = jnp.einsum('bqhw,bwd->bqhd', p_w.astype(vw.dtype), vw)
    out = g[..., 0:1] * o_c + g[..., 1:2] * o_s + g[..., 2:3] * o_w
    return out.reshape(B_, Tq, D_NSA)


def nsa_prompt(q, g, k_c, v_c, k_s, v_s, k_w, v_w, pe, w_ck, w_cv):
    B_, S = q.shape[:2]
    kc, vc, cmp_end, overlap = nsa_compress(k_c, v_c, pe, w_ck, w_cv)
    kbs, vbs = to_blocks(k_s), to_blocks(v_s)
    kw_pad = jnp.pad(k_w, ((0, 0), (WINDOW, 0), (0, 0)))
    vw_pad = jnp.pad(v_w, ((0, 0), (WINDOW, 0), (0, 0)))
    nblk = S // Q_BLOCK
    qb = jnp.moveaxis(q.reshape(B_, nblk, Q_BLOCK, NSA_HEADS, HEAD_DIM), 1, 0)
    gb = jnp.moveaxis(g.reshape(B_, nblk, Q_BLOCK, NSA_HEADS, 3), 1, 0)

    def one_block(args):
        q_i, g_i, i = args
        start = i * Q_BLOCK
        q_pos = start + jnp.arange(Q_BLOCK, dtype=jnp.int32)
        kw_i = lax.dynamic_slice_in_dim(kw_pad, start, WINDOW + Q_BLOCK, axis=1)
        vw_i = lax.dynamic_slice_in_dim(vw_pad, start, WINDOW + Q_BLOCK, axis=1)
        w_pos = start - WINDOW + jnp.arange(WINDOW + Q_BLOCK, dtype=jnp.int32)
        return nsa_attend(q_i, q_pos, g_i, kc, vc, cmp_end, overlap, kbs, vbs, kw_i, vw_i, w_pos)

    out = lax.map(one_block, (qb, gb, jnp.arange(nblk, dtype=jnp.int32)))
    return jnp.moveaxis(out, 0, 1).reshape(B_, S, D_NSA)


def fox_attend(q, q_pos, cq, k, v, k_pos, ck):
    B_, Tq = q.shape[:2]
    s = jnp.einsum('bqhd,bkhd->bhqk', q, k).astype(jnp.float32) * (HEAD_DIM ** -0.5)
    s = s + (jnp.swapaxes(cq, 1, 2)[..., :, None] - jnp.swapaxes(ck, 1, 2)[..., None, :])
    p = masked_softmax(s, (k_pos[None, :] <= q_pos[:, None])[None, None])
    o = jnp.einsum('bhqk,bkhd->bqhd', p.astype(v.dtype), v)
    return o.reshape(B_, Tq, D_FOX)


def fox_prompt(q, k, v, c):
    B_, S = q.shape[:2]
    nblk = S // Q_BLOCK
    k_pos = jnp.arange(S, dtype=jnp.int32)
    qb = jnp.moveaxis(q.reshape(B_, nblk, Q_BLOCK, FOX_HEADS, HEAD_DIM), 1, 0)
    cb = jnp.moveaxis(c.reshape(B_, nblk, Q_BLOCK, FOX_HEADS), 1, 0)

    def one_block(args):
        q_i, c_i, i = args
        q_pos = i * Q_BLOCK + jnp.arange(Q_BLOCK, dtype=jnp.int32)
        return fox_attend(q_i, q_pos, c_i, k, v, k_pos, c)

    out = lax.map(one_block, (qb, cb, jnp.arange(nblk, dtype=jnp.int32)))
    return jnp.moveaxis(out, 0, 1).reshape(B_, S, D_FOX)


def mixer_projections(h, pos, w_in_l, b_forget_l):
    B_, T = h.shape[:2]
    lru_x, lru_y, nsa_q, nsa_kv, nsa_g, fox_q, fox_k, fox_v, fox_f = jnp.split(h @ w_in_l, SPLIT_POINTS, axis=-1)
    q_n = rope(nsa_q.reshape(B_, T, NSA_HEADS, HEAD_DIM), pos)
    kv_n = nsa_kv.reshape(B_, T, 6, HEAD_DIM)
    k_n = rope(kv_n[:, :, 0::2], pos)
    v_n = kv_n[:, :, 1::2]
    g_n = jax.nn.sigmoid(nsa_g.reshape(B_, T, NSA_HEADS, 3))
    q_f = fox_q.reshape(B_, T, FOX_HEADS, HEAD_DIM)
    k_f = fox_k.reshape(B_, T, FOX_HEADS, HEAD_DIM)
    v_f = fox_v.reshape(B_, T, FOX_HEADS, HEAD_DIM)
    logf = jax.nn.log_sigmoid(fox_f.astype(jnp.float32) + b_forget_l.astype(jnp.float32))
    return lru_x, lru_y, q_n, k_n, v_n, g_n, q_f, k_f, v_f, logf


def mixer_output(o_lru, o_nsa, o_fox, g_grp_l, w_out_l):
    o = jnp.concatenate([
        rmsnorm(o_lru, g_grp_l[:LRU_WIDTH]),
        rmsnorm(o_nsa, g_grp_l[LRU_WIDTH:LRU_WIDTH + D_NSA]),
        rmsnorm(o_fox, g_grp_l[LRU_WIDTH + D_NSA:]),
    ], axis=-1)
    return o @ w_out_l


def channel_mixer(x, g_pre, g_post, w_up, w_down):
    h = rmsnorm(x, g_pre)
    return rmsnorm(jnp.square(jax.nn.relu(h @ w_up)) @ w_down, g_post)


def setup_inputs(seed: int = 0) -> dict:
    key = jax.random.key(seed)
    ks = jax.random.split(key, 40)
    f32 = jnp.float32

    def nrm(k, shape, scale=1.0):
        return jax.random.normal(k, shape, f32) * scale

    n_pages = PAST_LEN // PAGE_SIZE
    n_used = DEC_BATCH * n_pages
    n_phys = n_used + -(-n_used // 4)
    w_buf = min(WINDOW, PAST_LEN)
    page_table = jax.random.permutation(ks[0], n_phys)[:n_used].reshape(DEC_BATCH, n_pages).astype(jnp.int32)
    a0 = jax.random.uniform(ks[1], (DEPTH, LRU_WIDTH), f32, 0.9, 0.999)
    return {
        "x_prompt": nrm(ks[2], (BATCH, SEQ, D_MODEL)),
        "x_sample": nrm(ks[3], (DEC_BATCH, DEC_SEQ, D_MODEL)),
        "cache_fox_kv": nrm(ks[4], (DEPTH, n_phys, PAGE_SIZE, 2, FOX_HEADS, HEAD_DIM)),
        "cache_fox_logf": jax.nn.log_sigmoid(4.0 + nrm(ks[5], (DEPTH, n_phys, PAGE_SIZE, FOX_HEADS))),
        "cache_nsa_kv": nrm(ks[6], (DEPTH, n_phys, PAGE_SIZE, 4, HEAD_DIM)),
        "cache_nsa_win": nrm(ks[7], (DEPTH, DEC_BATCH, w_buf, 2, HEAD_DIM)),
        "state_lru_h": nrm(ks[8], (DEPTH, DEC_BATCH, LRU_WIDTH)),
        "state_lru_conv": nrm(ks[9], (DEPTH, DEC_BATCH, CONV_W - 1, LRU_WIDTH)),
        "page_table": page_table,
        "g_mix_pre": 1.0 + nrm(ks[10], (DEPTH, D_MODEL), 0.1),
        "g_mix_post": 1.0 + nrm(ks[11], (DEPTH, D_MODEL), 0.1),
        "g_grp": 1.0 + nrm(ks[12], (DEPTH, D_MIX), 0.1),
        "w_in": nrm(ks[13], (DEPTH, D_MODEL, D_IN), D_MODEL ** -0.5),
        "conv_w": nrm(ks[14], (DEPTH, CONV_W, LRU_WIDTH), CONV_W ** -0.5),
        "conv_b": nrm(ks[15], (DEPTH, LRU_WIDTH), 0.02),
        "w_rg": nrm(ks[16], (DEPTH, LRU_BLOCKS, LRU_BLOCK, LRU_BLOCK), LRU_BLOCK ** -0.5),
        "b_rg": nrm(ks[17], (DEPTH, LRU_WIDTH), 0.02),
        "w_ig": nrm(ks[18], (DEPTH, LRU_BLOCKS, LRU_BLOCK, LRU_BLOCK), LRU_BLOCK ** -0.5),
        "b_ig": nrm(ks[19], (DEPTH, LRU_WIDTH), 0.02),
        "lru_lambda": jnp.log(a0) - jnp.log1p(-a0),
        "cmp_pe": nrm(ks[20], (DEPTH, CMP_LEN, HEAD_DIM), 0.1),
        "w_cmp_k": nrm(ks[21], (DEPTH, CMP_LEN, HEAD_DIM, HEAD_DIM), (CMP_LEN * HEAD_DIM) ** -0.5),
        "w_cmp_v": nrm(ks[22], (DEPTH, CMP_LEN, HEAD_DIM, HEAD_DIM), (CMP_LEN * HEAD_DIM) ** -0.5),
        "b_forget": jax.random.uniform(ks[23], (DEPTH, FOX_HEADS), f32, 2.0, 6.0),
        "w_out": nrm(ks[24], (DEPTH, D_MIX, D_MODEL), D_MIX ** -0.5),
        "g_mlp_pre": 1.0 + nrm(ks[25], (DEPTH, D_MODEL), 0.1),
        "g_mlp_post": 1.0 + nrm(ks[26], (DEPTH, D_MODEL), 0.1),
        "w_up": nrm(ks[27], (DEPTH, D_MODEL, D_FF), D_MODEL ** -0.5),
        "w_down": nrm(ks[28], (DEPTH, D_FF, D_MODEL), D_FF ** -0.5),
    }


def reference(x_prompt, x_sample, cache_fox_kv, cache_fox_logf, cache_nsa_kv, cache_nsa_win,
              state_lru_h, state_lru_conv, page_table, g_mix_pre, g_mix_post, g_grp, w_in,
              conv_w, conv_b, w_rg, b_rg, w_ig, b_ig, lru_lambda, cmp_pe, w_cmp_k, w_cmp_v,
              b_forget, w_out, g_mlp_pre, g_mlp_post, w_up, w_down):
    xp, xs = x_prompt, x_sample
    B_p, S = xp.shape[:2]
    B_s, T_new = xs.shape[:2]
    past = page_table.shape[1] * cache_nsa_kv.shape[2]
    w_prompt = min(WINDOW, S)
    pos_p = jnp.arange(S, dtype=jnp.int32)
    pos_s = past + jnp.arange(T_new, dtype=jnp.int32)
    pf_kv, sf_kv, pf_lf, sf_lf, pn_kv, sn_kv, pn_w, sn_w, ph, sh, pc, sc = ([] for _ in range(12))
    for l in range(DEPTH):
        lru_w = (conv_w[l], conv_b[l], w_rg[l], b_rg[l], w_ig[l], b_ig[l], lru_lambda[l])
        h = rmsnorm(xp, g_mix_pre[l])
        lx, ly, qn, kn, vn, gn, qf, kf, vf, logf = mixer_projections(h, pos_p, w_in[l], b_forget[l])
        o_lru, conv_new, h_new = lru_mixer(lx, ly, jnp.zeros((B_p, CONV_W - 1, LRU_WIDTH), xp.dtype),
                                           jnp.zeros((B_p, LRU_WIDTH), xp.dtype), *lru_w)
        o_nsa = nsa_prompt(qn, gn, kn[:, :, 0], vn[:, :, 0], kn[:, :, 1], vn[:, :, 1], kn[:, :, 2], vn[:, :, 2],
                           cmp_pe[l], w_cmp_k[l], w_cmp_v[l])
        o_fox = fox_prompt(qf, kf, vf, jnp.cumsum(logf, axis=1))
        xp = xp + rmsnorm(mixer_output(o_lru, o_nsa, o_fox, g_grp[l], w_out[l]), g_mix_post[l])
        xp = xp + channel_mixer(xp, g_mlp_pre[l], g_mlp_post[l], w_up[l], w_down[l])
        pf_kv.append(jnp.stack([kf, vf], axis=2))
        pf_lf.append(logf)
        pn_kv.append(jnp.stack([kn[:, :, 0], vn[:, :, 0], kn[:, :, 1], vn[:, :, 1]], axis=2))
        pn_w.append(jnp.stack([kn[:, :, 2], vn[:, :, 2]], axis=2)[:, S - w_prompt:])
        ph.append(h_new)
        pc.append(conv_new)
        h = rmsnorm(xs, g_mix_pre[l])
        lx, ly, qn, kn, vn, gn, qf, kf, vf, logf = mixer_projections(h, pos_s, w_in[l], b_forget[l])
        o_lru, conv_new, h_new = lru_mixer(lx, ly, state_lru_conv[l], state_lru_h[l], *lru_w)
        nsa_past = gather_pages(cache_nsa_kv[l], page_table)
        k_c = jnp.concatenate([nsa_past[:, :, 0], kn[:, :, 0]], axis=1)
        v_c = jnp.concatenate([nsa_past[:, :, 1], vn[:, :, 0]], axis=1)
        k_s = jnp.concatenate([nsa_past[:, :, 2], kn[:, :, 1]], axis=1)
        v_s = jnp.concatenate([nsa_past[:, :, 3], vn[:, :, 1]], axis=1)
        win = cache_nsa_win[l]
        w_buf = win.shape[1]
        k_w = jnp.concatenate([win[:, :, 0], kn[:, :, 2]], axis=1)
        v_w = jnp.concatenate([win[:, :, 1], vn[:, :, 2]], axis=1)
        w_pos = past - w_buf + jnp.arange(w_buf + T_new, dtype=jnp.int32)
        kc, vc, cmp_end, overlap = nsa_compress(k_c, v_c, cmp_pe[l], w_cmp_k[l], w_cmp_v[l])
        o_nsa = nsa_attend(qn, pos_s, gn, kc, vc, cmp_end, overlap, to_blocks(k_s), to_blocks(v_s), k_w, v_w, w_pos)
        fox_past = gather_pages(cache_fox_kv[l], page_table)
        k_all = jnp.concatenate([fox_past[:, :, 0], kf], axis=1)
        v_all = jnp.concatenate([fox_past[:, :, 1], vf], axis=1)
        lf_all = jnp.concatenate([gather_pages(cache_fox_logf[l], page_table).astype(jnp.float32), logf], axis=1)
        c_all = jnp.cumsum(lf_all, axis=1)
        k_pos = jnp.arange(past + T_new, dtype=jnp.int32)
        o_fox = fox_attend(qf, pos_s, c_all[:, past:], k_all, v_all, k_pos, c_all)
        xs = xs + rmsnorm(mixer_output(o_lru, o_nsa, o_fox, g_grp[l], w_out[l]), g_mix_post[l])
        xs = xs + channel_mixer(xs, g_mlp_pre[l], g_mlp_post[l], w_up[l], w_down[l])
        sf_kv.append(jnp.stack([kf, vf], axis=2))
        sf_lf.append(logf)
        sn_kv.append(jnp.stack([kn[:, :, 0], vn[:, :, 0], kn[:, :, 1], vn[:, :, 1]], axis=2))
        sn_w.append(jnp.stack([kn[:, :, 2], vn[:, :, 2]], axis=2))
        sh.append(h_new)
        sc.append(conv_new)
    return (xp, xs,
            jnp.stack(pf_kv), jnp.stack(sf_kv),
            jnp.stack(pf_lf), jnp.stack(sf_lf),
            jnp.stack(pn_kv), jnp.stack(sn_kv),
            jnp.stack(pn_w), jnp.stack(sn_w),
            jnp.stack(ph), jnp.stack(sh),
            jnp.stack(pc), jnp.stack(sc))
```

```python
import functools

import numpy as np
import jax
import jax.numpy as jnp
from jax import lax
from jax.experimental import pallas as pl
from jax.experimental.pallas import tpu as pltpu

F32 = jnp.float32
BF16 = jnp.bfloat16

LANES = 128
SUBLANES = 8
VMEM_LIMIT = 56 * 1024 * 1024

D_MODEL = 2048
HEAD_DIM = 128
LRU_WIDTH = D_MODEL // 2
LRU_BLOCKS = 8
LRU_BLOCK = LRU_WIDTH // LRU_BLOCKS
CONV_W = 4
LRU_C = 8.0
NSA_HEADS = 4
FOX_HEADS = 4
D_NSA = NSA_HEADS * HEAD_DIM
D_FOX = FOX_HEADS * HEAD_DIM
CMP_LEN = 32
CMP_STRIDE = 16
SEL_BLOCK = 64
N_SELECT = 16
WINDOW = 512
FORCE_BONUS = 1e4
ROPE_DIM = HEAD_DIM // 4
ROPE_HALF = ROPE_DIM // 2
ROPE_THETA = 500000.0
D_FF = 4 * D_MODEL
EPS = 1e-6
NEG = -1e30
SCALE = HEAD_DIM ** -0.5

C_LX = 0
C_LY = C_LX + LRU_WIDTH
C_NQ = C_LY + LRU_WIDTH
C_FQ = C_NQ + D_NSA
C_FK = C_FQ + D_FOX
C_FV = C_FK + D_FOX
C_NKV = C_FV + D_FOX
D_PACK = C_NKV + 6 * HEAD_DIM
PROJ_TN = 256
N_LOGF = FOX_HEADS
N_GATE_USED = FOX_HEADS + 3 * NSA_HEADS

_O_NQ = 2 * LRU_WIDTH
_O_NKV = _O_NQ + D_NSA
_O_NG = _O_NKV + 6 * HEAD_DIM
_O_FQ = _O_NG + 3 * NSA_HEADS
_O_FK = _O_FQ + D_FOX
_O_FV = _O_FK + D_FOX
_O_FF = _O_FV + D_FOX


def _cparams(sem):
    return pltpu.CompilerParams(dimension_semantics=sem, vmem_limit_bytes=VMEM_LIMIT)


def _dot(a, b):
    return jnp.dot(a, b, preferred_element_type=F32)


def _dot_nt(a, b):
    return lax.dot_general(a, b, (((1,), (1,)), ((), ())), preferred_element_type=F32)


def _rms(x, g):
    return x * lax.rsqrt(jnp.mean(x * x, axis=-1, keepdims=True) + EPS) * g


def _softplus(x):
    return jnp.maximum(x, 0.0) + jnp.log1p(jnp.exp(-jnp.abs(x)))


def _gelu_tanh(x):
    c = np.float32(np.sqrt(2.0 / np.pi))
    return 0.5 * x * (1.0 + jnp.tanh(c * (x + 0.044715 * (x * x * x))))


def _split3(x):
    hi = x.astype(BF16)
    r1 = x - hi.astype(F32)
    mid = r1.astype(BF16)
    lo = (r1 - mid.astype(F32)).astype(BF16)
    return hi, mid, lo


def _rope(x, cos, sin_a, sin_b):
    return (x * cos + pltpu.roll(x, LANES - ROPE_HALF, 1) * sin_a
            + pltpu.roll(x, ROPE_HALF, 1) * sin_b)


_ROPE_BOTH = tuple(range(C_NQ // PROJ_TN, (C_NQ + D_NSA) // PROJ_TN))
_ROPE_FIRST = tuple(range(C_NKV // PROJ_TN, D_PACK // PROJ_TN))


def _inproj_kernel(x_ref, g_ref, w_ref, wg_ref, bg_ref, cos_ref, sa_ref, sb_ref,
                   proj_ref, pbf_ref, gate_ref, xn_ref):
    j = pl.program_id(1)

    @pl.when(j == 0)
    def _():
        x = x_ref[...]
        xn_ref[...] = _rms(x, g_ref[...]).astype(BF16)
        z = _dot(xn_ref[...], wg_ref[...]) + bg_ref[...]
        lane = lax.broadcasted_iota(jnp.int32, z.shape, 1)
        logsig = jnp.minimum(z, 0.0) - jnp.log1p(jnp.exp(-jnp.abs(z)))
        gate_ref[...] = jnp.where(lane < N_LOGF, logsig, jax.nn.sigmoid(z))

    acc = _dot(xn_ref[...], w_ref[...])
    is_both = functools.reduce(jnp.logical_or, [j == t for t in _ROPE_BOTH])
    is_first = functools.reduce(jnp.logical_or, [j == t for t in _ROPE_FIRST])

    def _store(val):
        proj_ref[...] = val
        pbf_ref[...] = val.astype(BF16)

    @pl.when(jnp.logical_not(jnp.logical_or(is_both, is_first)))
    def _():
        _store(acc)

    @pl.when(is_both)
    def _():
        cos, sa, sb = cos_ref[...], sa_ref[...], sb_ref[...]
        _store(jnp.concatenate([_rope(acc[:, :LANES], cos, sa, sb),
                                _rope(acc[:, LANES:], cos, sa, sb)], axis=1))

    @pl.when(is_first)
    def _():
        cos, sa, sb = cos_ref[...], sa_ref[...], sb_ref[...]
        _store(jnp.concatenate([_rope(acc[:, :LANES], cos, sa, sb), acc[:, LANES:]], axis=1))


def _inproj(x, g, w, wg, bg, cos, sa, sb, tm):
    m = x.shape[0]
    tm = min(tm, m)
    grid = (m // tm, D_PACK // PROJ_TN)
    row = lambda i, j: (i, 0)
    fixed = lambda i, j: (0, 0)
    return pl.pallas_call(
        _inproj_kernel,
        grid=grid,
        in_specs=[
            pl.BlockSpec((tm, D_MODEL), row),
            pl.BlockSpec((1, D_MODEL), fixed),
            pl.BlockSpec((D_MODEL, PROJ_TN), lambda i, j: (0, j)),
            pl.BlockSpec((D_MODEL, LANES), fixed),
            pl.BlockSpec((1, LANES), fixed),
            pl.BlockSpec((tm, LANES), row),
            pl.BlockSpec((tm, LANES), row),
            pl.BlockSpec((tm, LANES), row),
        ],
        out_specs=[
            pl.BlockSpec((tm, PROJ_TN), lambda i, j: (i, j)),
            pl.BlockSpec((tm, PROJ_TN), lambda i, j: (i, j)),
            pl.BlockSpec((tm, LANES), row),
        ],
        out_shape=[
            jax.ShapeDtypeStruct((m, D_PACK), F32),
            jax.ShapeDtypeStruct((m, D_PACK), BF16),
            jax.ShapeDtypeStruct((m, LANES), F32),
        ],
        scratch_shapes=[pltpu.VMEM((tm, D_MODEL), BF16)],
        compiler_params=_cparams(("arbitrary", "arbitrary")),
        name="inproj",
    )(x, g, w, wg, bg, cos, sa, sb)


def _lru_gates(xc, wg_ref, brg, big, lam):
    xcb = xc.astype(BF16)
    rs, is_ = [], []
    for g in range(LRU_BLOCKS):
        z = _dot(xcb[:, g * LRU_BLOCK:(g + 1) * LRU_BLOCK], wg_ref[g])
        rs.append(z[:, :LRU_BLOCK])
        is_.append(z[:, LRU_BLOCK:])
    r = jax.nn.sigmoid(jnp.concatenate(rs, axis=1) + brg)
    i = jax.nn.sigmoid(jnp.concatenate(is_, axis=1) + big)
    log_a = -LRU_C * r * _softplus(-lam)
    a = jnp.exp(log_a)
    u = jnp.sqrt(1.0 - a * a) * (i * xc)
    return a, u


def _lru_prompt_kernel(lx_ref, ly_ref, cw_ref, cb_ref, wg_ref, brg_ref, big_ref, lam_ref,
                       o_ref, hlast_ref, conv_ref, ext_ref, h_ref):
    t = pl.program_id(1)
    nt = pl.num_programs(1)
    tc = lx_ref.shape[0]

    @pl.when(t == 0)
    def _():
        ext_ref[0:SUBLANES, :] = jnp.zeros((SUBLANES, LRU_WIDTH), F32)
        h_ref[...] = jnp.zeros_like(h_ref)

    x = lx_ref[...]
    ext_ref[SUBLANES:, :] = x
    xc = cb_ref[...] + cw_ref[CONV_W - 1:CONV_W, :] * x
    for k in range(CONV_W - 1):
        xc = xc + cw_ref[k:k + 1, :] * ext_ref[pl.ds(SUBLANES - (CONV_W - 1) + k, tc), :]

    @pl.when(t == nt - 1)
    def _():
        conv_ref[...] = ext_ref[pl.ds(tc + SUBLANES - (CONV_W - 1), CONV_W - 1), :]

    ext_ref[0:SUBLANES, :] = x[tc - SUBLANES:, :]

    a, u = _lru_gates(xc, wg_ref, brg_ref[...], big_ref[...], lam_ref[...])
    rowi = lax.broadcasted_iota(jnp.int32, a.shape, 0)
    s = 1
    while s < tc:
        keep = rowi >= s
        a_sh = jnp.where(keep, pltpu.roll(a, s, 0), 1.0)
        u_sh = jnp.where(keep, pltpu.roll(u, s, 0), 0.0)
        u = u + a * u_sh
        a = a * a_sh
        s *= 2
    hs = u + a * h_ref[0:1, :]
    h_ref[...] = jnp.broadcast_to(hs[tc - 1:tc, :], h_ref.shape)
    hlast_ref[...] = hs[tc - 1:tc, :]
    o_ref[...] = hs * _gelu_tanh(ly_ref[...])


def _lru_prompt(proj, b, s, cw, cb, wg, brg, big, lam, tc=256):
    nt = s // tc
    vec = lambda bi, t: (0, 0)
    return pl.pallas_call(
        _lru_prompt_kernel,
        grid=(b, nt),
        in_specs=[
            pl.BlockSpec((tc, LRU_WIDTH), lambda bi, t: (bi * nt + t, C_LX // LRU_WIDTH)),
            pl.BlockSpec((tc, LRU_WIDTH), lambda bi, t: (bi * nt + t, C_LY // LRU_WIDTH)),
            pl.BlockSpec((CONV_W, LRU_WIDTH), vec),
            pl.BlockSpec((1, LRU_WIDTH), vec),
            pl.BlockSpec((LRU_BLOCKS, LRU_BLOCK, 2 * LRU_BLOCK), lambda bi, t: (0, 0, 0)),
            pl.BlockSpec((1, LRU_WIDTH), vec),
            pl.BlockSpec((1, LRU_WIDTH), vec),
            pl.BlockSpec((1, LRU_WIDTH), vec),
        ],
        out_specs=[
            pl.BlockSpec((tc, LRU_WIDTH), lambda bi, t: (bi * nt + t, 0)),
            pl.BlockSpec((None, 1, LRU_WIDTH), lambda bi, t: (bi, 0, 0)),
            pl.BlockSpec((None, CONV_W - 1, LRU_WIDTH), lambda bi, t: (bi, 0, 0)),
        ],
        out_shape=[
            jax.ShapeDtypeStruct((b * s, LRU_WIDTH), F32),
            jax.ShapeDtypeStruct((b, 1, LRU_WIDTH), F32),
            jax.ShapeDtypeStruct((b, CONV_W - 1, LRU_WIDTH), F32),
        ],
        scratch_shapes=[pltpu.VMEM((tc + SUBLANES, LRU_WIDTH), F32),
                        pltpu.VMEM((SUBLANES, LRU_WIDTH), F32)],
        compiler_params=_cparams(("arbitrary", "arbitrary")),
        name="lru_prompt",
    )(proj, proj, cw, cb, wg, brg, big, lam)


def _lru_sample_kernel(lx_ref, ly_ref, buf_ref, h0_ref, cw_ref, cb_ref, wg_ref, brg_ref, big_ref,
                       lam_ref, o_ref, h_ref):
    x = lx_ref[...]
    xc = cb_ref[...] + cw_ref[CONV_W - 1:CONV_W, :] * x
    for k in range(CONV_W - 1):
        xc = xc + cw_ref[k:k + 1, :] * buf_ref[k]
    a, u = _lru_gates(xc, wg_ref, brg_ref[...], big_ref[...], lam_ref[...])
    h = a * h0_ref[...] + u
    h_ref[...] = h
    o_ref[...] = h * _gelu_tanh(ly_ref[...])


def _lru_sample(proj, buf_t, h0, cw, cb, wg, brg, big, lam):
    nb = proj.shape[0]
    vec = lambda i: (0, 0)
    return pl.pallas_call(
        _lru_sample_kernel,
        grid=(1,),
        in_specs=[
            pl.BlockSpec((nb, LRU_WIDTH), lambda i: (0, C_LX // LRU_WIDTH)),
            pl.BlockSpec((nb, LRU_WIDTH), lambda i: (0, C_LY // LRU_WIDTH)),
            pl.BlockSpec((CONV_W - 1, nb, LRU_WIDTH), lambda i: (0, 0, 0)),
            pl.BlockSpec((nb, LRU_WIDTH), vec),
            pl.BlockSpec((CONV_W, LRU_WIDTH), vec),
            pl.BlockSpec((1, LRU_WIDTH), vec),
            pl.BlockSpec((LRU_BLOCKS, LRU_BLOCK, 2 * LRU_BLOCK), lambda i: (0, 0, 0)),
            pl.BlockSpec((1, LRU_WIDTH), vec),
            pl.BlockSpec((1, LRU_WIDTH), vec),
            pl.BlockSpec((1, LRU_WIDTH), vec),
        ],
        out_specs=[pl.BlockSpec((nb, LRU_WIDTH), vec), pl.BlockSpec((nb, LRU_WIDTH), vec)],
        out_shape=[jax.ShapeDtypeStruct((nb, LRU_WIDTH), F32),
                   jax.ShapeDtypeStruct((nb, LRU_WIDTH), F32)],
        compiler_params=_cparams(("arbitrary",)),
        name="lru_sample",
    )(proj, proj, buf_t, h0, cw, cb, wg, brg, big, lam)


def _cumsum_kernel(g_ref, ccol_ref, crow_ref):
    x = g_ref[...]
    n = x.shape[0]
    rowi = lax.broadcasted_iota(jnp.int32, x.shape, 0)
    s = 1
    while s < n:
        x = x + jnp.where(rowi >= s, pltpu.roll(x, s, 0), 0.0)
        s *= 2
    ccol_ref[...] = x
    crow_ref[...] = x.T[0:SUBLANES, :]


def _fox_cumsum(gates, b, s):
    return pl.pallas_call(
        _cumsum_kernel,
        grid=(b,),
        in_specs=[pl.BlockSpec((s, LANES), lambda bi: (bi, 0))],
        out_specs=[pl.BlockSpec((s, LANES), lambda bi: (bi, 0)),
                   pl.BlockSpec((None, SUBLANES, s), lambda bi: (bi, 0, 0))],
        out_shape=[jax.ShapeDtypeStruct((b * s, LANES), F32),
                   jax.ShapeDtypeStruct((b, SUBLANES, s), F32)],
        compiler_params=_cparams(("arbitrary",)),
        name="fox_cumsum",
    )(gates)


def _flash_update(carry, s, mask, v):
    m, l, acc = carry
    if mask is not None:
        s = jnp.where(mask, s, NEG)
    m_new = jnp.maximum(m, jnp.max(s, axis=-1, keepdims=True))
    p = jnp.exp(s - m_new)
    if mask is not None:
        p = jnp.where(mask, p, 0.0)
    alpha = jnp.exp(m - m_new)
    l = alpha * l + jnp.sum(p, axis=-1, keepdims=True)
    acc = alpha * acc + _dot(p.astype(BF16), v)
    return m_new, l, acc


def _flash_init(rows, d):
    return (jnp.full((rows, 1), NEG, F32), jnp.zeros((rows, 1), F32), jnp.zeros((rows, d), F32))


def _flash_finish(carry):
    _, l, acc = carry
    return acc / jnp.maximum(l, 1e-30)


def _fox_prompt_kernel(q_ref, k_ref, v_ref, ccol_ref, crow_ref, o_ref):
    qi = pl.program_id(1)
    tq = q_ref.shape[0]
    rowi = lax.broadcasted_iota(jnp.int32, (tq, tq), 0)
    coli = lax.broadcasted_iota(jnp.int32, (tq, tq), 1)
    causal = coli <= rowi
    for h in range(FOX_HEADS):
        hs = slice(h * HEAD_DIM, (h + 1) * HEAD_DIM)
        q = q_ref[:, hs]
        cq = ccol_ref[:, h:h + 1]

        def step(kj, carry, mask, q=q, cq=cq, hs=hs, h=h):
            ks = pl.multiple_of(kj * tq, tq)
            k = k_ref[pl.ds(ks, tq), hs]
            v = v_ref[pl.ds(ks, tq), hs]
            s = _dot_nt(q, k) * SCALE + (cq - crow_ref[h:h + 1, pl.ds(ks, tq)])
            return _flash_update(carry, s, mask, v)

        carry = lax.fori_loop(0, qi, lambda kj, c: step(kj, c, None), _flash_init(tq, HEAD_DIM))
        carry = step(qi, carry, causal)
        o_ref[:, hs] = _flash_finish(carry)


def _fox_prompt(pbf, ccol, crow, b, s, tq=256):
    nq = s // tq
    return pl.pallas_call(
        _fox_prompt_kernel,
        grid=(b, nq),
        in_specs=[
            pl.BlockSpec((tq, D_FOX), lambda bi, qi: (bi * nq + qi, C_FQ // D_FOX)),
            pl.BlockSpec((s, D_FOX), lambda bi, qi: (bi, C_FK // D_FOX)),
            pl.BlockSpec((s, D_FOX), lambda bi, qi: (bi, C_FV // D_FOX)),
            pl.BlockSpec((tq, LANES), lambda bi, qi: (bi * nq + qi, 0)),
            pl.BlockSpec((None, SUBLANES, s), lambda bi, qi: (bi, 0, 0)),
        ],
        out_specs=pl.BlockSpec((tq, D_FOX), lambda bi, qi: (bi * nq + qi, 0)),
        out_shape=jax.ShapeDtypeStruct((b * s, D_FOX), F32),
        compiler_params=_cparams(("arbitrary", "arbitrary")),
        name="fox_prompt",
    )(pbf, pbf, pbf, ccol, crow)


def _compress_rows(load_group, pe_ref, w_ref, rows):
    top = jnp.zeros((rows, HEAD_DIM), F32)
    bot = jnp.zeros((rows, HEAD_DIM), F32)
    for l in range(CMP_STRIDE):
        x = load_group(l)
        w = w_ref[l]
        top = top + _dot((x + pe_ref[l:l + 1, :]).astype(BF16), w[:, :HEAD_DIM])
        bot = bot + _dot((x + pe_ref[CMP_STRIDE + l:CMP_STRIDE + l + 1, :]).astype(BF16), w[:, HEAD_DIM:])
    return top + pltpu.roll(bot, rows - 1, 0)


def _nsa_compress_kernel(k_ref, v_ref, pe_ref, wk_ref, wv_ref, kc_ref, vc_ref):
    groups = k_ref.shape[0] // CMP_STRIDE
    kc_ref[...] = _compress_rows(lambda l: k_ref[pl.ds(l, groups, stride=CMP_STRIDE), :],
                                 pe_ref, wk_ref, groups).astype(BF16)
    vc_ref[...] = _compress_rows(lambda l: v_ref[pl.ds(l, groups, stride=CMP_STRIDE), :],
                                 pe_ref, wv_ref, groups).astype(BF16)


def _nsa_compress(proj, b, s, pe, wk2, wv2):
    groups = s // CMP_STRIDE
    wspec = pl.BlockSpec((CMP_STRIDE, HEAD_DIM, 2 * HEAD_DIM), lambda bi: (0, 0, 0))
    return pl.pallas_call(
        _nsa_compress_kernel,
        grid=(b,),
        in_specs=[
            pl.BlockSpec((s, HEAD_DIM), lambda bi: (bi, C_NKV // HEAD_DIM)),
            pl.BlockSpec((s, HEAD_DIM), lambda bi: (bi, C_NKV // HEAD_DIM + 1)),
            pl.BlockSpec((CMP_LEN, HEAD_DIM), lambda bi: (0, 0)),
            wspec, wspec,
        ],
        out_specs=[pl.BlockSpec((groups, HEAD_DIM), lambda bi: (bi, 0)),
                   pl.BlockSpec((groups, HEAD_DIM), lambda bi: (bi, 0))],
        out_shape=[jax.ShapeDtypeStruct((b * groups, HEAD_DIM), BF16),
                   jax.ShapeDtypeStruct((b * groups, HEAD_DIM), BF16)],
        compiler_params=_cparams(("arbitrary",)),
        name="nsa_compress",
    )(proj, proj, pe, wk2, wv2)


def _masked_softmax(s, mask):
    s = jnp.where(mask, s, NEG)
    m = jnp.max(s, axis=-1, keepdims=True)
    e = jnp.where(mask, jnp.exp(s - m), 0.0)
    return e / jnp.maximum(jnp.sum(e, axis=-1, keepdims=True), 1e-30)


def _select_blocks(imp, cur, n_iter):
    blk = lax.broadcasted_iota(jnp.int32, imp.shape, 1)
    forced = (blk == 0) | (blk == cur) | (blk == cur - 1)
    score = jnp.where(blk <= cur, imp + jnp.where(forced, FORCE_BONUS, 0.0), NEG)
    cnt = jnp.zeros(imp.shape, F32)
    for i in range(n_iter):
        si = score[:, i:i + 1]
        beats = (si > score) | ((si == score) & (blk > i))
        cnt = cnt + jnp.where(beats, 1.0, 0.0)
    return jnp.where((cnt < N_SELECT) & (score > NEG / 2), 1.0, 0.0)


def _nsa_prompt_kernel(q_ref, g_ref, kc_ref, vc_ref, ov_ref, ks_ref, vs_ref, kw_ref, vw_ref, o_ref):
    qi = pl.program_id(1)
    tq = q_ref.shape[0]
    n_cmp_pad = kc_ref.shape[0]
    n_blk = ks_ref.shape[0] // SEL_BLOCK
    q_pos = qi * tq + lax.broadcasted_iota(jnp.int32, (tq, 1), 0)

    kc = kc_ref[...]
    vc = vc_ref[...]
    cmp_end = lax.broadcasted_iota(jnp.int32, (tq, n_cmp_pad), 1) * CMP_STRIDE + (CMP_LEN - 1)
    cmp_mask = cmp_end <= q_pos
    o_cmp = []
    p_sum = jnp.zeros((tq, n_cmp_pad), F32)
    for h in range(NSA_HEADS):
        q = q_ref[:, h * HEAD_DIM:(h + 1) * HEAD_DIM]
        p = _masked_softmax(_dot_nt(q, kc) * SCALE, cmp_mask)
        o_cmp.append(_dot(p.astype(BF16), vc))
        p_sum = p_sum + p
    hi, mid, lo = _split3(p_sum)
    ov = ov_ref[...]
    imp = _dot(hi, ov) + _dot(mid, ov) + _dot(lo, ov)
    sel = _select_blocks(imp, q_pos // SEL_BLOCK, n_blk).astype(BF16)

    q4 = jnp.concatenate([q_ref[:, h * HEAD_DIM:(h + 1) * HEAD_DIM] for h in range(NSA_HEADS)], axis=0)
    qp4 = jnp.concatenate([q_pos] * NSA_HEADS, axis=0)
    blk_row = lax.broadcasted_iota(jnp.int32, (LANES, tq), 0)
    tok_col = lax.broadcasted_iota(jnp.int32, (LANES, tq), 1)
    k_off = lax.broadcasted_iota(jnp.int32, (NSA_HEADS * tq, tq), 1)

    def sel_step(kj, carry):
        ks = pl.multiple_of(kj * tq, tq)
        expand = jnp.where(blk_row == (ks + tok_col) // SEL_BLOCK, 1.0, 0.0).astype(BF16)
        member = _dot(sel, expand)
        member4 = jnp.concatenate([member] * NSA_HEADS, axis=0)
        mask = (member4 > 0.5) & (ks + k_off <= qp4)
        s = _dot_nt(q4, ks_ref[pl.ds(ks, tq), :]) * SCALE
        return _flash_update(carry, s, mask, vs_ref[pl.ds(ks, tq), :])

    o_sel = _flash_finish(lax.fori_loop(0, qi + 1, sel_step, _flash_init(NSA_HEADS * tq, HEAD_DIM)))

    def win_step(kj, carry):
        ks = pl.multiple_of(kj * tq, tq)
        dist = qp4 - (ks + k_off)
        mask = (dist >= 0) & (dist < WINDOW)
        s = _dot_nt(q4, kw_ref[pl.ds(ks, tq), :]) * SCALE
        return _flash_update(carry, s, mask, vw_ref[pl.ds(ks, tq), :])

    lo_tile = jnp.maximum(qi - WINDOW // tq, 0)
    o_win = _flash_finish(lax.fori_loop(lo_tile, qi + 1, win_step, _flash_init(NSA_HEADS * tq, HEAD_DIM)))

    g = g_ref[...]
    for h in range(NSA_HEADS):
        c0 = N_LOGF + 3 * h
        rows = slice(h * tq, (h + 1) * tq)
        o_ref[:, h * HEAD_DIM:(h + 1) * HEAD_DIM] = (
            g[:, c0:c0 + 1] * o_cmp[h] + g[:, c0 + 1:c0 + 2] * o_sel[rows] + g[:, c0 + 2:c0 + 3] * o_win[rows])


def _nsa_prompt(pbf, gates, kc, vc, ov, b, s, tq=128):
    nq = s // tq
    groups = s // CMP_STRIDE
    kvspec = lambda c: pl.BlockSpec((s, HEAD_DIM), lambda bi, qi: (bi, C_NKV // HEAD_DIM + c))
    return pl.pallas_call(
        _nsa_prompt_kernel,
        grid=(b, nq),
        in_specs=[
            pl.BlockSpec((tq, D_NSA), lambda bi, qi: (bi * nq + qi, C_NQ // D_NSA)),
            pl.BlockSpec((tq, LANES), lambda bi, qi: (bi * nq + qi, 0)),
            pl.BlockSpec((groups, HEAD_DIM), lambda bi, qi: (bi, 0)),
            pl.BlockSpec((groups, HEAD_DIM), lambda bi, qi: (bi, 0)),
            pl.BlockSpec((groups, LANES), lambda bi, qi: (0, 0)),
            kvspec(2), kvspec(3), kvspec(4), kvspec(5),
        ],
        out_specs=pl.BlockSpec((tq, D_NSA), lambda bi, qi: (bi * nq + qi, 0)),
        out_shape=jax.ShapeDtypeStruct((b * s, D_NSA), F32),
        compiler_params=_cparams(("arbitrary", "arbitrary")),
        name="nsa_prompt",
    )(pbf, gates, kc, vc, ov, pbf, pbf, pbf, pbf)


def _overlap_matrix(n_rows, n_cmp, n_slc):
    starts = np.arange(n_rows) * CMP_STRIDE
    j_start = np.arange(LANES) * SEL_BLOCK
    ov = (starts[:, None] < j_start[None, :] + SEL_BLOCK) & (starts[:, None] + CMP_LEN > j_start[None, :])
    ov &= (np.arange(n_rows) < n_cmp)[:, None] & (np.arange(LANES) < n_slc)[None, :]
    return jnp.asarray(ov.astype(np.float32), dtype=BF16)


def _mixout_kernel(ol_ref, on_ref, of_ref, gg_ref, w_ref, x_ref, gp_ref, o_ref):
    gg = gg_ref[...]
    nl = _rms(ol_ref[...], gg[:, :LRU_WIDTH]).astype(BF16)
    nn = _rms(on_ref[...], gg[:, LRU_WIDTH:LRU_WIDTH + D_NSA]).astype(BF16)
    nf = _rms(of_ref[...], gg[:, LRU_WIDTH + D_NSA:]).astype(BF16)
    y = (_dot(nl, w_ref[0:LRU_WIDTH, :]) + _dot(nn, w_ref[LRU_WIDTH:LRU_WIDTH + D_NSA, :])
         + _dot(nf, w_ref[LRU_WIDTH + D_NSA:, :]))
    o_ref[...] = x_ref[...] + _rms(y, gp_ref[...])


def _mixout(o_lru, o_nsa, o_fox, g_grp, w_out, x, g_post, tm=512):
    m = x.shape[0]
    tm = min(tm, m)
    row = lambda i: (i, 0)
    fixed = lambda i: (0, 0)
    return pl.pallas_call(
        _mixout_kernel,
        grid=(m // tm,),
        in_specs=[
            pl.BlockSpec((tm, LRU_WIDTH), row),
            pl.BlockSpec((tm, D_NSA), row),
            pl.BlockSpec((tm, D_FOX), row),
            pl.BlockSpec((1, D_MODEL), fixed),
            pl.BlockSpec((D_MODEL, D_MODEL), fixed),
            pl.BlockSpec((tm, D_MODEL), row),
            pl.BlockSpec((1, D_MODEL), fixed),
        ],
        out_specs=pl.BlockSpec((tm, D_MODEL), row),
        out_shape=jax.ShapeDtypeStruct((m, D_MODEL), F32),
        compiler_params=_cparams(("arbitrary",)),
        name="mixout",
    )(o_lru, o_nsa, o_fox, g_grp, w_out, x, g_post)


def _ffn_kernel(x_ref, gpre_ref, wu_ref, wd_ref, gpost_ref, o_ref, xn_ref, acc_ref):
    f = pl.program_id(1)

    @pl.when(f == 0)
    def _():
        xn_ref[...] = _rms(x_ref[...], gpre_ref[...]).astype(BF16)
        acc_ref[...] = jnp.zeros_like(acc_ref)

    up = jnp.maximum(_dot(xn_ref[...], wu_ref[...]), 0.0)
    acc_ref[...] += _dot((up * up).astype(BF16), wd_ref[...])

    @pl.when(f == pl.num_programs(1) - 1)
    def _():
        o_ref[...] = x_ref[...] + _rms(acc_ref[...], gpost_ref[...])


def _ffn(x, g_pre, w_up, w_down, g_post, tm=512, tf=512):
    m = x.shape[0]
    tm = min(tm, m)
    row = lambda i, f: (i, 0)
    fixed = lambda i, f: (0, 0)
    return pl.pallas_call(
        _ffn_kernel,
        grid=(m // tm, D_FF // tf),
        in_specs=[
            pl.BlockSpec((tm, D_MODEL), row),
            pl.BlockSpec((1, D_MODEL), fixed),
            pl.BlockSpec((D_MODEL, tf), lambda i, f: (0, f)),
            pl.BlockSpec((tf, D_MODEL), lambda i, f: (f, 0)),
            pl.BlockSpec((1, D_MODEL), fixed),
        ],
        out_specs=pl.BlockSpec((tm, D_MODEL), row),
        out_shape=jax.ShapeDtypeStruct((m, D_MODEL), F32),
        scratch_shapes=[pltpu.VMEM((tm, D_MODEL), BF16), pltpu.VMEM((tm, D_MODEL), F32)],
        compiler_params=_cparams(("arbitrary", "arbitrary")),
        name="ffn",
    )(x, g_pre, w_up, w_down, g_post)


NSA_BB = 4
FOX_BB = 2


def _softmax_with_new(s, mask, s_new, new_valid):
    s = jnp.where(mask, s, NEG)
    s_new = jnp.where(new_valid, s_new, NEG)
    m = jnp.maximum(jnp.max(s, axis=-1, keepdims=True), s_new)
    e = jnp.where(mask, jnp.exp(s - m), 0.0)
    e_new = jnp.where(new_valid, jnp.exp(s_new - m), 0.0)
    denom = jnp.maximum(jnp.sum(e, axis=-1, keepdims=True) + e_new, 1e-30)
    return e, e_new, denom


def _bf16_round(x):
    return x.astype(BF16).astype(F32)


def _nsa_decode_kernel(n_pages, page_size, pt_ref, *refs):
    n_in = NSA_BB * n_pages
    kc_pages = refs[:n_in]
    vc_pages = refs[n_in:2 * n_in]
    sl_pages = refs[2 * n_in:3 * n_in]
    (q_ref, kvn_ref, g_ref, win_ref, pe_ref, wk_ref, wv_ref, ov_ref, ex_ref,
     o_ref, ks_ref, vs_ref) = refs[3 * n_in:]
    past = n_pages * page_size
    gpp = page_size // CMP_STRIDE
    rows = n_pages * gpp
    n_blk = past // SEL_BLOCK + 1
    q_pos = past

    def loader(pages):
        return lambda l: jnp.concatenate([pg[pl.ds(l, gpp, stride=CMP_STRIDE), :] for pg in pages], axis=0)

    kc_all = _compress_rows(loader(kc_pages), pe_ref, wk_ref, NSA_BB * rows).astype(BF16)
    vc_all = _compress_rows(loader(vc_pages), pe_ref, wv_ref, NSA_BB * rows).astype(BF16)

    for bb in range(NSA_BB):
        for p in range(n_pages):
            pg = sl_pages[bb * n_pages + p]
            ks_ref[bb, p * page_size:(p + 1) * page_size, :] = pg[:, 0:HEAD_DIM].astype(BF16)
            vs_ref[bb, p * page_size:(p + 1) * page_size, :] = pg[:, HEAD_DIM:2 * HEAD_DIM].astype(BF16)

    cmp_end = lax.broadcasted_iota(jnp.int32, (SUBLANES, rows), 1) * CMP_STRIDE + (CMP_LEN - 1)
    cmp_mask = cmp_end <= q_pos
    head_row = lax.broadcasted_iota(jnp.int32, (SUBLANES, 1), 0) < NSA_HEADS
    o_cmp, p_rows = [], []
    for bb in range(NSA_BB):
        q8 = q_ref[bb]
        kc = kc_all[bb * rows:(bb + 1) * rows]
        vc = vc_all[bb * rows:(bb + 1) * rows]
        p = _masked_softmax(_dot_nt(q8, kc) * SCALE, cmp_mask)
        o_cmp.append(_dot(p.astype(BF16), vc))
        p_rows.append(jnp.sum(jnp.where(head_row, p, 0.0), axis=0, keepdims=True))
    p_rows.append(jnp.zeros((SUBLANES - NSA_BB, rows), F32))
    p_sum = jnp.concatenate(p_rows, axis=0)
    hi, mid, lo = _split3(p_sum)
    ov = ov_ref[...]
    imp = _dot(hi, ov) + _dot(mid, ov) + _dot(lo, ov)
    cur = jnp.full((SUBLANES, 1), q_pos // SEL_BLOCK, jnp.int32)
    sel = _select_blocks(imp, cur, n_blk)
    member = _dot(sel.astype(BF16), ex_ref[...])

    w_len = win_ref.shape[1]
    w_pos = past - w_len + lax.broadcasted_iota(jnp.int32, (SUBLANES, w_len), 1)
    dist = q_pos - w_pos
    win_mask = (dist >= 0) & (dist < WINDOW) & (w_pos >= 0)
    for bb in range(NSA_BB):
        q8 = q_ref[bb]
        q8f = q8.astype(F32)
        kvn = kvn_ref[bb]
        g = g_ref[bb]
        s = _dot_nt(q8, ks_ref[bb]) * SCALE
        mask = jnp.broadcast_to(member[bb:bb + 1, :] > 0.5, s.shape)
        k_new = _bf16_round(kvn[:, 2 * HEAD_DIM:3 * HEAD_DIM])
        s_new = jnp.sum(q8f * k_new, axis=-1, keepdims=True) * SCALE
        new_valid = jnp.broadcast_to(sel[bb:bb + 1, n_blk - 1:n_blk] > 0.5, s_new.shape)
        e, e_new, den = _softmax_with_new(s, mask, s_new, new_valid)
        v_new = _bf16_round(kvn[:, 3 * HEAD_DIM:4 * HEAD_DIM])
        o_sel = (_dot(e.astype(BF16), vs_ref[bb]) + _bf16_round(e_new) * v_new) / den
        kw = win_ref[bb, :, 0:HEAD_DIM].astype(BF16)
        vw = win_ref[bb, :, HEAD_DIM:2 * HEAD_DIM].astype(BF16)
        s = _dot_nt(q8, kw) * SCALE
        k_new = _bf16_round(kvn[:, 4 * HEAD_DIM:5 * HEAD_DIM])
        s_new = jnp.sum(q8f * k_new, axis=-1, keepdims=True) * SCALE
        e, e_new, den = _softmax_with_new(s, win_mask, s_new, jnp.full(s_new.shape, True))
        v_new = _bf16_round(kvn[:, 5 * HEAD_DIM:6 * HEAD_DIM])
        o_win = (_dot(e.astype(BF16), vw) + _bf16_round(e_new) * v_new) / den
        o_ref[bb] = g[:, 0:1] * o_cmp[bb] + g[:, 1:2] * o_sel + g[:, 2:3] * o_win


def _nsa_decode(page_table, cache, q8, kvn, g8, win, pe, wk2, wv2, ov, ex):
    nb, n_pages = page_table.shape
    page_size = cache.shape[1]
    past = n_pages * page_size
    steps = nb // NSA_BB

    def page_spec(bb, p, width, col):
        return pl.BlockSpec((None, page_size, width), lambda i, pt: (pt[i * NSA_BB + bb, p], 0, col))

    idx = [(bb, p) for bb in range(NSA_BB) for p in range(n_pages)]
    page_specs = ([page_spec(bb, p, HEAD_DIM, 0) for bb, p in idx]
                  + [page_spec(bb, p, HEAD_DIM, 1) for bb, p in idx]
                  + [page_spec(bb, p, 2 * HEAD_DIM, 1) for bb, p in idx])

    fixed2 = lambda i, pt: (0, 0)
    fixed3 = lambda i, pt: (0, 0, 0)
    batch3 = lambda i, pt: (i, 0, 0)
    wspec = pl.BlockSpec((CMP_STRIDE, HEAD_DIM, 2 * HEAD_DIM), fixed3)
    grid_spec = pltpu.PrefetchScalarGridSpec(
        num_scalar_prefetch=1,
        grid=(steps,),
        in_specs=page_specs + [
            pl.BlockSpec((NSA_BB, SUBLANES, HEAD_DIM), batch3),
            pl.BlockSpec((NSA_BB, 1, 6 * HEAD_DIM), batch3),
            pl.BlockSpec((NSA_BB, SUBLANES, LANES), batch3),
            pl.BlockSpec((NSA_BB, win.shape[1], 2 * HEAD_DIM), batch3),
            pl.BlockSpec((CMP_LEN, HEAD_DIM), fixed2),
            wspec, wspec,
            pl.BlockSpec(ov.shape, fixed2),
            pl.BlockSpec(ex.shape, fixed2),
        ],
        out_specs=pl.BlockSpec((NSA_BB, SUBLANES, HEAD_DIM), batch3),
        scratch_shapes=[pltpu.VMEM((NSA_BB, past, HEAD_DIM), BF16),
                        pltpu.VMEM((NSA_BB, past, HEAD_DIM), BF16)],
    )
    return pl.pallas_call(
        functools.partial(_nsa_decode_kernel, n_pages, page_size),
        grid_spec=grid_spec,
        out_shape=jax.ShapeDtypeStruct((nb, SUBLANES, HEAD_DIM), F32),
        compiler_params=_cparams(("arbitrary",)),
        name="nsa_decode",
    )(page_table, *([cache] * (3 * NSA_BB * n_pages)), q8, kvn, g8, win, pe, wk2, wv2, ov, ex)


def _fox_decode_kernel(n_pages, page_size, pt_ref, *refs):
    n_in = FOX_BB * n_pages
    kv_pages = refs[:n_in]
    lf_pages = refs[n_in:2 * n_in]
    q_ref, kvn_ref, lfn_ref, tri_ref, o_ref = refs[2 * n_in:]
    past = n_pages * page_size
    lane_blk = lax.broadcasted_iota(jnp.int32, (SUBLANES, D_FOX), 1) // HEAD_DIM
    row_id = lax.broadcasted_iota(jnp.int32, (SUBLANES, D_FOX), 0)
    diag = lane_blk == row_id
    tri = tri_ref[...]
    zpad = jnp.zeros((SUBLANES - FOX_HEADS, page_size), F32)
    for bb in range(FOX_BB):
        q_bdf = jnp.where(diag, jnp.broadcast_to(q_ref[bb].astype(F32), (SUBLANES, D_FOX)), 0.0)
        q_bd = q_bdf.astype(BF16)
        s_pages, c_pages = [], []
        carry = jnp.zeros((SUBLANES, 1), F32)
        for p in range(n_pages):
            pg = kv_pages[bb * n_pages + p]
            s_pages.append(_dot_nt(q_bd, pg[:, 0:D_FOX].astype(BF16)))
            lf = jnp.concatenate([lf_pages[bb * n_pages + p][...], zpad], axis=0)
            hi, mid, lo = _split3(lf)
            cp = _dot(hi, tri) + _dot(mid, tri) + _dot(lo, tri) + carry
            c_pages.append(cp)
            carry = cp[:, page_size - 1:page_size]
        s = jnp.concatenate(s_pages, axis=1) * SCALE
        c = jnp.concatenate(c_pages, axis=1)
        cq = carry + lfn_ref[bb][:, 0:1]
        s = s + (cq - c)
        kvn = kvn_ref[bb]
        k_new = _bf16_round(kvn[:, 0:D_FOX])
        s_new = jnp.sum(q_bdf * k_new, axis=-1, keepdims=True) * SCALE + (cq - cq)
        k_pos = lax.broadcasted_iota(jnp.int32, s.shape, 1)
        e, e_new, den = _softmax_with_new(s, k_pos <= past, s_new, jnp.full(s_new.shape, True))
        eb = e.astype(BF16)
        o = _bf16_round(e_new) * _bf16_round(kvn[:, D_FOX:2 * D_FOX])
        for p in range(n_pages):
            pg = kv_pages[bb * n_pages + p]
            o = o + _dot(eb[:, p * page_size:(p + 1) * page_size], pg[:, D_FOX:2 * D_FOX].astype(BF16))
        o = o / den
        o_ref[bb] = jnp.sum(jnp.where(diag, o, 0.0), axis=0, keepdims=True)


def _fox_decode(page_table, cache_kv, cache_lf_t, q, kvn, lfn, tri):
    nb, n_pages = page_table.shape
    page_size = cache_kv.shape[1]
    steps = nb // FOX_BB

    def kv_spec(bb, p):
        return pl.BlockSpec((None, page_size, 2 * D_FOX), lambda i, pt: (pt[i * FOX_BB + bb, p], 0, 0))

    def lf_spec(bb, p):
        return pl.BlockSpec((None, FOX_HEADS, page_size), lambda i, pt: (pt[i * FOX_BB + bb, p], 0, 0))

    batch3 = lambda i, pt: (i, 0, 0)
    idx = [(bb, p) for bb in range(FOX_BB) for p in range(n_pages)]
    grid_spec = pltpu.PrefetchScalarGridSpec(
        num_scalar_prefetch=1,
        grid=(steps,),
        in_specs=[kv_spec(bb, p) for bb, p in idx] + [lf_spec(bb, p) for bb, p in idx] + [
            pl.BlockSpec((FOX_BB, 1, D_FOX), batch3),
            pl.BlockSpec((FOX_BB, 1, 2 * D_FOX), batch3),
            pl.BlockSpec((FOX_BB, SUBLANES, LANES), batch3),
            pl.BlockSpec((page_size, page_size), lambda i, pt: (0, 0)),
        ],
        out_specs=pl.BlockSpec((FOX_BB, 1, D_FOX), batch3),
    )
    n_in = FOX_BB * n_pages
    return pl.pallas_call(
        functools.partial(_fox_decode_kernel, n_pages, page_size),
        grid_spec=grid_spec,
        out_shape=jax.ShapeDtypeStruct((nb, 1, D_FOX), F32),
        compiler_params=_cparams(("arbitrary",)),
        name="fox_decode",
    )(page_table, *([cache_kv] * n_in), *([cache_lf_t] * n_in), q, kvn, lfn, tri)


def _rope_tables(pos):
    inv = ROPE_THETA ** (-jnp.arange(ROPE_HALF, dtype=F32) / ROPE_HALF)
    ang = pos.astype(F32)[:, None] * inv[None, :]
    cos, sin = jnp.cos(ang), jnp.sin(ang)
    n = pos.shape[0]
    rest = LANES - ROPE_DIM
    cos_t = jnp.concatenate([cos, cos, jnp.ones((n, rest), F32)], axis=1)
    sin_a = jnp.concatenate([-sin, jnp.zeros((n, LANES - ROPE_HALF), F32)], axis=1)
    sin_b = jnp.concatenate([jnp.zeros((n, ROPE_HALF), F32), sin, jnp.zeros((n, rest), F32)], axis=1)
    return cos_t, sin_a, sin_b


def _pack_w_in(w, b_forget):
    main = jnp.concatenate([w[:, :_O_NKV], w[:, _O_FQ:_O_FF], w[:, _O_NKV:_O_NG]], axis=1).astype(BF16)
    pad = jnp.zeros((w.shape[0], LANES - N_GATE_USED), w.dtype)
    gate = jnp.concatenate([w[:, _O_FF:], w[:, _O_NG:_O_FQ], pad], axis=1).astype(BF16)
    bias = jnp.concatenate([b_forget.astype(F32), jnp.zeros((LANES - N_LOGF,), F32)])[None, :]
    return main, gate, bias


def kernel(x_prompt, x_sample, cache_fox_kv, cache_fox_logf, cache_nsa_kv, cache_nsa_win, state_lru_h, state_lru_conv, page_table, g_mix_pre, g_mix_post, g_grp, w_in, conv_w, conv_b, w_rg, b_rg, w_ig, b_ig, lru_lambda, cmp_pe, w_cmp_k, w_cmp_v, b_forget, w_out, g_mlp_pre, g_mlp_post, w_up, w_down):
    bp, s, _ = x_prompt.shape
    nb, t_new, _ = x_sample.shape
    assert t_new == 1
    depth = w_in.shape[0]
    n_phys, page_size = cache_nsa_kv.shape[1:3]
    n_pages = page_table.shape[1]
    past = n_pages * page_size
    w_buf = cache_nsa_win.shape[2]
    mp = bp * s

    xp = x_prompt.reshape(mp, D_MODEL)
    xs = x_sample.reshape(nb, D_MODEL)
    cos_p, sa_p, sb_p = _rope_tables(jnp.tile(jnp.arange(s, dtype=jnp.int32), bp))
    cos_s, sa_s, sb_s = _rope_tables(jnp.full((nb,), past, jnp.int32))

    n_cmp_p = (s - CMP_LEN) // CMP_STRIDE + 1
    ov_p = _overlap_matrix(s // CMP_STRIDE, n_cmp_p, -(-s // SEL_BLOCK))
    n_cmp_s = (past + 1 - CMP_LEN) // CMP_STRIDE + 1
    n_slc_s = -(-(past + 1) // SEL_BLOCK)
    ov_s = _overlap_matrix(past // CMP_STRIDE, n_cmp_s, n_slc_s)
    expand_s = jnp.asarray(
        (np.arange(LANES)[:, None] == (np.arange(past)[None, :] // SEL_BLOCK)).astype(np.float32), dtype=BF16)
    tri = jnp.asarray(np.triu(np.ones((page_size, page_size), np.float32)), dtype=BF16)

    outs = [[] for _ in range(12)]
    w_prompt = min(WINDOW, s)
    for l in range(depth):
        w_main, w_gate, b_gate = _pack_w_in(w_in[l], b_forget[l])
        wg_lru = jnp.concatenate([w_rg[l], w_ig[l]], axis=-1).astype(BF16)
        wk2 = jnp.concatenate([w_cmp_k[l][:CMP_STRIDE], w_cmp_k[l][CMP_STRIDE:]], axis=-1).astype(BF16)
        wv2 = jnp.concatenate([w_cmp_v[l][:CMP_STRIDE], w_cmp_v[l][CMP_STRIDE:]], axis=-1).astype(BF16)
        w_out_l = w_out[l].astype(BF16)
        w_up_l = w_up[l].astype(BF16)
        w_down_l = w_down[l].astype(BF16)
        g_pre, g_post, g_grp_l = g_mix_pre[l][None, :], g_mix_post[l][None, :], g_grp[l][None, :]
        lru_vecs = (conv_w[l], conv_b[l][None, :], wg_lru, b_rg[l][None, :], b_ig[l][None, :],
                    lru_lambda[l][None, :])

        proj, pbf, gates = _inproj(xp, g_pre, w_main, w_gate, b_gate, cos_p, sa_p, sb_p, tm=1024)
        o_lru, h_last, conv_new = _lru_prompt(proj, bp, s, *lru_vecs)
        ccol, crow = _fox_cumsum(gates, bp, s)
        o_fox = _fox_prompt(pbf, ccol, crow, bp, s)
        kc, vc = _nsa_compress(proj, bp, s, cmp_pe[l], wk2, wv2)
        o_nsa = _nsa_prompt(pbf, gates, kc, vc, ov_p, bp, s)
        xp = _mixout(o_lru, o_nsa, o_fox, g_grp_l, w_out_l, xp, g_post)
        xp = _ffn(xp, g_mlp_pre[l][None, :], w_up_l, w_down_l, g_mlp_post[l][None, :])
        outs[0].append(proj[:, C_FK:C_FK + 2 * D_FOX].reshape(bp, s, 2, FOX_HEADS, HEAD_DIM))
        outs[2].append(gates[:, :N_LOGF].reshape(bp, s, FOX_HEADS))
        outs[4].append(proj[:, C_NKV:C_NKV + 4 * HEAD_DIM].reshape(bp, s, 4, HEAD_DIM))
        outs[6].append(proj[:, C_NKV + 4 * HEAD_DIM:].reshape(bp, s, 2, HEAD_DIM)[:, s - w_prompt:])
        outs[8].append(h_last.reshape(bp, LRU_WIDTH))
        outs[10].append(conv_new)

        proj, pbf, gates = _inproj(xs, g_pre, w_main, w_gate, b_gate, cos_s, sa_s, sb_s, tm=nb)
        buf_t = jnp.swapaxes(state_lru_conv[l], 0, 1)
        o_lru, h_new = _lru_sample(proj, buf_t, state_lru_h[l], *lru_vecs)
        q8 = jnp.pad(pbf[:, C_NQ:C_NQ + D_NSA].reshape(nb, NSA_HEADS, HEAD_DIM),
                     ((0, 0), (0, SUBLANES - NSA_HEADS), (0, 0)))
        g8 = jnp.pad(gates[:, N_LOGF:N_GATE_USED].reshape(nb, NSA_HEADS, 3),
                     ((0, 0), (0, SUBLANES - NSA_HEADS), (0, LANES - 3)))
        kvn = proj[:, C_NKV:].reshape(nb, 1, 6 * HEAD_DIM)
        o_nsa = _nsa_decode(page_table, cache_nsa_kv[l].reshape(n_phys, page_size, 4 * HEAD_DIM),
                            q8, kvn, g8, cache_nsa_win[l].reshape(nb, w_buf, 2 * HEAD_DIM),
                            cmp_pe[l], wk2, wv2, ov_s, expand_s)
        o_nsa = o_nsa[:, :NSA_HEADS].reshape(nb, D_NSA)
        lfn = jnp.pad(jnp.broadcast_to(gates[:, :N_LOGF, None], (nb, N_LOGF, LANES)),
                      ((0, 0), (0, SUBLANES - N_LOGF), (0, 0)))
        o_fox = _fox_decode(page_table, cache_fox_kv[l].reshape(n_phys, page_size, 2 * D_FOX),
                            jnp.swapaxes(cache_fox_logf[l], 1, 2).astype(F32),
                            pbf[:, C_FQ:C_FQ + D_FOX].reshape(nb, 1, D_FOX),
                            proj[:, C_FK:C_FK + 2 * D_FOX].reshape(nb, 1, 2 * D_FOX), lfn, tri)
        o_fox = o_fox.reshape(nb, D_FOX)
        xs = _mixout(o_lru, o_nsa, o_fox, g_grp_l, w_out_l, xs, g_post)
        xs = _ffn(xs, g_mlp_pre[l][None, :], w_up_l, w_down_l, g_mlp_post[l][None, :])
        outs[1].append(proj[:, C_FK:C_FK + 2 * D_FOX].reshape(nb, 1, 2, FOX_HEADS, HEAD_DIM))
        outs[3].append(gates[:, :N_LOGF].reshape(nb, 1, FOX_HEADS))
        outs[5].append(proj[:, C_NKV:C_NKV + 4 * HEAD_DIM].reshape(nb, 1, 4, HEAD_DIM))
        outs[7].append(proj[:, C_NKV + 4 * HEAD_DIM:].reshape(nb, 1, 2, HEAD_DIM))
        outs[9].append(h_new)
        outs[11].append(jnp.concatenate([state_lru_conv[l][:, 1:], proj[:, None, C_LX:C_LX + LRU_WIDTH]], axis=1))

    return (xp.reshape(bp, s, D_MODEL), xs.reshape(nb, 1, D_MODEL),
            *[jnp.stack(o) for o in outs])
```
